```python
import jax, jax.numpy as jnp
from jax import lax
import numpy as np

D_MODEL = 2048
BATCH = 32
SEQ = 256
DEPTH = 1
DEC_BATCH = 4
DEC_SEQ = 1024
PAST_LEN = 256

GRID_W = 64
GLA_HEADS = 4
GLA_DK = 128
GLA_DV = 256
GLA_KEY_WIDTH = GLA_HEADS * GLA_DK
GLA_VAL_WIDTH = GLA_HEADS * GLA_DV
GLA_RANK = 16
GLA_TAU = 16.0
FOURIER_GROUPS = 4
FOURIER_GROUP_DIM = 256
FOURIER_WIDTH = FOURIER_GROUPS * FOURIER_GROUP_DIM
IN_COLS = 2 * GLA_KEY_WIDTH + 2 * GLA_VAL_WIDTH + 2 * GLA_RANK + FOURIER_WIDTH
FFN_HIDDEN = ((8 * D_MODEL + 3 * 256 - 1) // (3 * 256)) * 256
N_MOD = 6
EPS = 1e-6

kernel_name = "hybrid_gla_fnet_diffusion_step"


def rmsnorm(x, g):
    xf = x.astype(jnp.float32)
    y = xf * lax.rsqrt(jnp.mean(xf * xf, axis=-1, keepdims=True) + EPS)
    return (y * g.astype(jnp.float32)).astype(x.dtype)


def gla_chunked(q, k, v, log_a, s0, n_chunks):
    bsz, nh, t, dk = q.shape
    dv = v.shape[-1]
    c = t // n_chunks
    qc = q.reshape(bsz, nh, n_chunks, c, dk)
    kc = k.reshape(bsz, nh, n_chunks, c, dk)
    vc = v.reshape(bsz, nh, n_chunks, c, dv)
    lc = log_a.reshape(bsz, nh, n_chunks, c, dk)
    b = jnp.cumsum(lc, axis=3)
    total = b[:, :, :, -1:, :]
    q_in = qc * jnp.exp(b)
    k_in = kc * jnp.exp(-b)
    k_out = kc * jnp.exp(total - b)
    mask = jnp.tril(jnp.ones((c, c), dtype=bool))
    att = jnp.where(mask, jnp.einsum('bhnid,bhnjd->bhnij', q_in, k_in), 0.0)
    o_intra = jnp.einsum('bhnij,bhnjv->bhniv', att, vc)
    decay = jnp.exp(total[:, :, :, 0, :])

    def step(state, xs):
        qi, ko, vi, dec = xs
        o = jnp.einsum('bhid,bhdv->bhiv', qi, state)
        state = state * dec[..., None] + jnp.einsum('bhid,bhiv->bhdv', ko, vi)
        return state, o

    xs = (jnp.moveaxis(q_in, 2, 0), jnp.moveaxis(k_out, 2, 0),
          jnp.moveaxis(vc, 2, 0), jnp.moveaxis(decay, 2, 0))
    s_fin, o_inter = lax.scan(step, s0, xs)
    o = o_intra + jnp.moveaxis(o_inter, 0, 2)
    return o.reshape(bsz, nh, t, dv), s_fin


def mixer(h, s0_f, s0_b, n_chunks, lw):
    bsz, t, _ = h.shape
    proj = h @ lw['w_in']
    splits = np.cumsum([GLA_KEY_WIDTH, GLA_KEY_WIDTH, GLA_VAL_WIDTH, GLA_VAL_WIDTH,
                        GLA_RANK, GLA_RANK]).tolist()
    q, k, v, g, lr_f, lr_b, fo = jnp.split(proj, splits, axis=-1)

    def heads(z, d):
        return z.astype(jnp.float32).reshape(bsz, t, GLA_HEADS, d).transpose(0, 2, 1, 3)

    qh = heads(q, GLA_DK) * (GLA_DK ** -0.5)
    kh = heads(k, GLA_DK)
    vh = heads(v, GLA_DV)
    la_f = heads(jax.nn.log_sigmoid((lr_f @ lw['w_a2_fwd'] + lw['b_a_fwd']).astype(jnp.float32)) / GLA_TAU, GLA_DK)
    la_b = heads(jax.nn.log_sigmoid((lr_b @ lw['w_a2_bwd'] + lw['b_a_bwd']).astype(jnp.float32)) / GLA_TAU, GLA_DK)
    o_f, s_f = gla_chunked(qh, kh, vh, la_f, s0_f.astype(jnp.float32), n_chunks)
    flip = lambda z: jnp.flip(z, axis=2)
    o_b, s_b = gla_chunked(flip(qh), flip(kh), flip(vh), flip(la_b), s0_b.astype(jnp.float32), n_chunks)
    o = o_f + flip(o_b)
    o = o * lax.rsqrt(jnp.mean(o * o, axis=-1, keepdims=True) + EPS)
    o = o * lw['gla_out_norm'].astype(jnp.float32)[None, :, None, :]
    o = o.transpose(0, 2, 1, 3).reshape(bsz, t, GLA_VAL_WIDTH)
    o = (o * jax.nn.silu(g.astype(jnp.float32))).astype(h.dtype)

    fg = fo.astype(jnp.float32).reshape(bsz, t, FOURIER_GROUPS, FOURIER_GROUP_DIM)
    f = jnp.fft.fft2(fg, axes=(1, 3), norm='ortho').real
    f = f.reshape(bsz, t, FOURIER_WIDTH).astype(h.dtype)

    gates = jax.nn.sigmoid(h @ lw['w_gate'] + lw['b_gate'])
    g_a, g_b = jnp.split(gates, 2, axis=-1)
    merged = g_a * (o @ lw['w_br_gla']) + g_b * (f @ lw['w_br_four'])
    return merged @ lw['w_out'], s_f, s_b


def layer(x, cond, s0_f, s0_b, n_chunks, lw):
    mod = jax.nn.silu(cond) @ lw['w_ada'] + lw['b_ada']
    sh1, sc1, ga1, sh2, sc2, ga2 = [m[:, None, :] for m in jnp.split(mod, N_MOD, axis=-1)]
    h = rmsnorm(x, lw['norm_pre_mix']) * (1.0 + sc1) + sh1
    m, s_f, s_b = mixer(h, s0_f, s0_b, n_chunks, lw)
    x = x + ga1 * rmsnorm(m, lw['norm_post_mix'])
    h = rmsnorm(x, lw['norm_pre_ffn']) * (1.0 + sc2) + sh2
    ff = (jax.nn.silu(h @ lw['w_ffn_gate']) * (h @ lw['w_ffn_up'])) @ lw['w_ffn_down']
    x = x + ga2 * rmsnorm(ff, lw['norm_post_ffn'])
    return x, s_f, s_b


def setup_inputs(seed: int = 0) -> dict:
    key = jax.random.key(seed)
    ks = jax.random.split(key, 32)
    f32 = jnp.float32
    nrm = lambda i, shape, s: (jax.random.normal(ks[i], shape, f32) * s)
    gain = lambda i, shape: 1.0 + 0.05 * jax.random.normal(ks[i], shape, f32)
    st_shape = (DEC_BATCH, DEPTH, GLA_HEADS, GLA_DK, GLA_DV)
    return {
        'x_prompt': nrm(0, (BATCH, SEQ, D_MODEL), 1.0),
        'x_sample': nrm(1, (DEC_BATCH, DEC_SEQ, D_MODEL), 1.0),
        'state_gla_fwd': nrm(2, st_shape, 1.0),
        'state_gla_bwd': nrm(3, st_shape, 1.0),
        'c': nrm(4, (DEC_BATCH, D_MODEL), 1.0),
        'c_ctx': nrm(5, (D_MODEL,), 1.0),
        'w_ada': nrm(6, (DEPTH, D_MODEL, N_MOD * D_MODEL), D_MODEL ** -0.5),
        'b_ada': nrm(7, (DEPTH, N_MOD * D_MODEL), 0.02),
        'norm_pre_mix': gain(8, (DEPTH, D_MODEL)),
        'norm_post_mix': gain(9, (DEPTH, D_MODEL)),
        'norm_pre_ffn': gain(10, (DEPTH, D_MODEL)),
        'norm_post_ffn': gain(11, (DEPTH, D_MODEL)),
        'w_in': nrm(12, (DEPTH, D_MODEL, IN_COLS), D_MODEL ** -0.5),
        'w_a2_fwd': nrm(13, (DEPTH, GLA_RANK, GLA_KEY_WIDTH), GLA_RANK ** -0.5),
        'b_a_fwd': nrm(14, (DEPTH, GLA_KEY_WIDTH), 0.1),
        'w_a2_bwd': nrm(15, (DEPTH, GLA_RANK, GLA_KEY_WIDTH), GLA_RANK ** -0.5),
        'b_a_bwd': nrm(16, (DEPTH, GLA_KEY_WIDTH), 0.1),
        'gla_out_norm': gain(17, (DEPTH, GLA_HEADS, GLA_DV)),
        'w_br_gla': nrm(18, (DEPTH, GLA_VAL_WIDTH, D_MODEL), GLA_VAL_WIDTH ** -0.5),
        'w_br_four': nrm(19, (DEPTH, FOURIER_WIDTH, D_MODEL), FOURIER_WIDTH ** -0.5),
        'w_gate': nrm(20, (DEPTH, D_MODEL, 2 * D_MODEL), D_MODEL ** -0.5),
        'b_gate': nrm(21, (DEPTH, 2 * D_MODEL), 0.02),
        'w_out': nrm(22, (DEPTH, D_MODEL, D_MODEL), D_MODEL ** -0.5),
        'w_ffn_gate': nrm(23, (DEPTH, D_MODEL, FFN_HIDDEN), D_MODEL ** -0.5),
        'w_ffn_up': nrm(24, (DEPTH, D_MODEL, FFN_HIDDEN), D_MODEL ** -0.5),
        'w_ffn_down': nrm(25, (DEPTH, FFN_HIDDEN, D_MODEL), FFN_HIDDEN ** -0.5),
    }


def reference(x_prompt, x_sample, state_gla_fwd, state_gla_bwd, c, c_ctx, w_ada, b_ada,
              norm_pre_mix, norm_post_mix, norm_pre_ffn, norm_post_ffn, w_in,
              w_a2_fwd, b_a_fwd, w_a2_bwd, b_a_bwd, gla_out_norm, w_br_gla, w_br_four,
              w_gate, b_gate, w_out, w_ffn_gate, w_ffn_up, w_ffn_down):
    ctx_chunks = x_prompt.shape[1] // GRID_W
    rows = x_sample.shape[1] // GRID_W
    bp = x_prompt.shape[0]
    zero_state = jnp.zeros((bp, GLA_HEADS, GLA_DK, GLA_DV), jnp.float32)
    cond_ctx = c_ctx[None, :]
    xp, xs = x_prompt, x_sample
    new_f, new_b = [], []
    for l in range(DEPTH):
        lw = dict(w_ada=w_ada[l], b_ada=b_ada[l], norm_pre_mix=norm_pre_mix[l],
                  norm_post_mix=norm_post_mix[l], norm_pre_ffn=norm_pre_ffn[l],
                  norm_post_ffn=norm_post_ffn[l], w_in=w_in[l], w_a2_fwd=w_a2_fwd[l],
                  b_a_fwd=b_a_fwd[l], w_a2_bwd=w_a2_bwd[l], b_a_bwd=b_a_bwd[l],
                  gla_out_norm=gla_out_norm[l], w_br_gla=w_br_gla[l], w_br_four=w_br_four[l],
                  w_gate=w_gate[l], b_gate=b_gate[l], w_out=w_out[l],
                  w_ffn_gate=w_ffn_gate[l], w_ffn_up=w_ffn_up[l], w_ffn_down=w_ffn_down[l])
        xp, s_f, s_b = layer(xp, cond_ctx, zero_state, zero_state, ctx_chunks, lw)
        new_f.append(s_f)
        new_b.append(s_b)
        xs, _, _ = layer(xs, c, state_gla_fwd[:, l], state_gla_bwd[:, l], rows, lw)
    new_state_gla_fwd = jnp.stack(new_f, axis=1)
    new_state_gla_bwd = jnp.stack(new_b, axis=1)
    return (xp, xs, new_state_gla_fwd, new_state_gla_bwd)
```

```python
import functools
import math

import numpy as np
import jax
import jax.numpy as jnp
from jax import lax
from jax.experimental import pallas as pl
from jax.experimental.pallas import tpu as pltpu

_F32 = jnp.float32
_BF16 = jnp.bfloat16

EPS = 1e-6
N_MOD = 6
GLA_HEADS = 4
GLA_DK = 128
GLA_DV = 256
GLA_RANK = 16
GLA_CHUNK = 64
GLA_INV_TAU = 1.0 / 16.0
FOURIER_GROUPS = 4
FOURIER_GROUP_DIM = 256

V7X_VMEM_BYTES = 64 * 1024 * 1024
VMEM_LIMIT_BYTES = 56 * 1024 * 1024
LANES = 128


def _dot(a, b):
    return jnp.dot(a, b, preferred_element_type=_F32)


def _dot_nt(a, b):
    return lax.dot_general(a, b, (((1,), (1,)), ((), ())), preferred_element_type=_F32)


def _dot_tn(a, b):
    return lax.dot_general(a, b, (((0,), (0,)), ((), ())), preferred_element_type=_F32)


def _rms_scale(x):
    return x * lax.rsqrt(jnp.mean(x * x, axis=-1, keepdims=True) + EPS)


def _mod_rmsnorm(x, gain, shift, scale):
    return (_rms_scale(x) * gain) * (1.0 + scale) + shift


def _silu(x):
    return x * jax.nn.sigmoid(x)


def _params(semantics):
    return pltpu.CompilerParams(dimension_semantics=semantics, vmem_limit_bytes=VMEM_LIMIT_BYTES)


def _ada_kernel(c_ref, w_ref, b_ref, o_ref):
    s = _silu(c_ref[...]).astype(_BF16)
    o_ref[...] = _dot(s, w_ref[...].astype(_BF16)) + b_ref[...]


def _ada(cond, w_ada, b_ada, *, tn=1024):
    rows, d = cond.shape
    n = w_ada.shape[1]
    return pl.pallas_call(
        _ada_kernel,
        grid=(n // tn,),
        in_specs=[
            pl.BlockSpec((rows, d), lambda j: (0, 0)),
            pl.BlockSpec((d, tn), lambda j: (0, j)),
            pl.BlockSpec((1, tn), lambda j: (0, j)),
        ],
        out_specs=pl.BlockSpec((rows, tn), lambda j: (0, j)),
        out_shape=jax.ShapeDtypeStruct((rows, n), _F32),
        compiler_params=_params(("parallel",)),
        name="ada",
    )(cond, w_ada, b_ada.reshape(1, n))


def _inproj_kernel(x_ref, mod_ref, g_ref, w_ref, wlr_ref, o_ref, lr_ref, h_s):
    @pl.when(pl.program_id(1) == 0)
    def _():
        h = _mod_rmsnorm(x_ref[...], g_ref[...], mod_ref[0, 0:1, :], mod_ref[0, 1:2, :]).astype(_BF16)
        h_s[...] = h
        lr_ref[...] = _dot(h, wlr_ref[...])

    o_ref[...] = _dot(h_s[...], w_ref[...])


def _in_proj(x2, mod, gain, w_main, w_lr, *, tm, tn=512):
    m, d = x2.shape
    n = w_main.shape[1]
    tiles_per_mod = (m // mod.shape[0]) // tm
    return pl.pallas_call(
        _inproj_kernel,
        grid=(m // tm, n // tn),
        in_specs=[
            pl.BlockSpec((tm, d), lambda i, j: (i, 0)),
            pl.BlockSpec((1, N_MOD, d), lambda i, j: (i // tiles_per_mod, 0, 0)),
            pl.BlockSpec((1, d), lambda i, j: (0, 0)),
            pl.BlockSpec((d, tn), lambda i, j: (0, j)),
            pl.BlockSpec((d, LANES), lambda i, j: (0, 0)),
        ],
        out_specs=[
            pl.BlockSpec((tm, tn), lambda i, j: (i, j)),
            pl.BlockSpec((tm, LANES), lambda i, j: (i, 0)),
        ],
        out_shape=[jax.ShapeDtypeStruct((m, n), _F32), jax.ShapeDtypeStruct((m, LANES), _F32)],
        scratch_shapes=[pltpu.VMEM((tm, d), _BF16)],
        compiler_params=_params(("parallel", "arbitrary")),
        name="in_proj",
    )(x2, mod, gain.reshape(1, d), w_main, w_lr)


def _log_sigmoid(x):
    return jnp.minimum(x, 0.0) - jnp.log1p(jnp.exp(-jnp.abs(x)))


def _chunk_cumsum(x, reverse):
    t = x.shape[0]
    pos = lax.broadcasted_iota(jnp.int32, x.shape, 0) & (GLA_CHUNK - 1)
    s = 1
    while s < GLA_CHUNK:
        if reverse:
            shifted = pltpu.roll(x, t - s, axis=0)
            keep = pos < GLA_CHUNK - s
        else:
            shifted = pltpu.roll(x, s, axis=0)
            keep = pos >= s
        x = x + jnp.where(keep, shifted, 0.0)
        s *= 2
    return x


def _gla_kernel(*refs, has_s0, emit_state, n_chunks):
    it = iter(refs)
    q_ref, k_ref, v_ref, g_ref, lr_ref = (next(it) for _ in range(5))
    waf_ref, wab_ref, baf_ref, bab_ref, onorm_ref = (next(it) for _ in range(5))
    s0f_ref, s0b_ref = (next(it), next(it)) if has_s0 else (None, None)
    o_ref = next(it)
    sf_ref, sb_ref = (next(it), next(it)) if emit_state else (None, None)
    bf_s, bb_s, of_s, ob_s, stf_s, stb_s = (next(it) for _ in range(6))

    c = GLA_CHUNK
    lr = lr_ref[0].astype(_BF16)
    bf_s[...] = _chunk_cumsum(_log_sigmoid(_dot(lr, waf_ref[0]) + baf_ref[0]) * GLA_INV_TAU, False)
    bb_s[...] = _chunk_cumsum(_log_sigmoid(_dot(lr, wab_ref[0]) + bab_ref[0]) * GLA_INV_TAU, True)

    if has_s0:
        stf_s[...] = s0f_ref[0, 0].T
        stb_s[...] = s0b_ref[0, 0].T
    else:
        stf_s[...] = jnp.zeros_like(stf_s)
        stb_s[...] = jnp.zeros_like(stb_s)

    row = lax.broadcasted_iota(jnp.int32, (c, c), 0)
    col = lax.broadcasted_iota(jnp.int32, (c, c), 1)
    q_scale = GLA_DK ** -0.5

    def one_chunk(r0, b_s, st_s, out_s, mask, total_row):
        rows = pl.ds(r0, c)
        b = b_s[rows, :]
        total = b[total_row:total_row + 1, :]
        q = q_ref[0, rows, :] * q_scale
        k = k_ref[0, rows, :]
        v = v_ref[0, rows, :].astype(_BF16)
        qi = (q * jnp.exp(b)).astype(_BF16)
        ki = (k * jnp.exp(-b)).astype(_BF16)
        ko = (k * jnp.exp(total - b)).astype(_BF16)
        att = jnp.where(mask, _dot_nt(qi, ki), 0.0).astype(_BF16)
        st = st_s[...]
        out_s[rows, :] = _dot(att, v) + _dot_nt(qi, st.astype(_BF16))
        st_s[...] = st * jnp.exp(total) + _dot_tn(v, ko)

    def body(n, carry):
        one_chunk(pl.multiple_of(n * c, c), bf_s, stf_s, of_s, col <= row, c - 1)
        one_chunk(pl.multiple_of((n_chunks - 1 - n) * c, c), bb_s, stb_s, ob_s, col >= row, 0)
        return carry

    lax.fori_loop(0, n_chunks, body, 0)

    o = of_s[...] + ob_s[...]
    o = _rms_scale(o) * onorm_ref[0]
    o_ref[0] = (o * _silu(g_ref[0])).astype(o_ref.dtype)
    if emit_state:
        sf_ref[0, 0] = stf_s[...].T
        sb_ref[0, 0] = stb_s[...].T


def _gla(proj3, lr3, lw, s0f, s0b, *, emit_state):
    bsz, t, _ = proj3.shape
    nh, dk, dv = GLA_HEADS, GLA_DK, GLA_DV
    has_s0 = s0f is not None
    k_blk0 = (nh * dk) // dk
    v_blk0 = (2 * nh * dk) // dv
    g_blk0 = v_blk0 + nh
    st_spec = pl.BlockSpec((1, 1, dk, dv), lambda b, h: (b, h, 0, 0))
    in_specs = [
        pl.BlockSpec((1, t, dk), lambda b, h: (b, 0, h)),
        pl.BlockSpec((1, t, dk), lambda b, h: (b, 0, k_blk0 + h)),
        pl.BlockSpec((1, t, dv), lambda b, h: (b, 0, v_blk0 + h)),
        pl.BlockSpec((1, t, dv), lambda b, h: (b, 0, g_blk0 + h)),
        pl.BlockSpec((1, t, LANES), lambda b, h: (b, 0, 0)),
        pl.BlockSpec((1, LANES, dk), lambda b, h: (h, 0, 0)),
        pl.BlockSpec((1, LANES, dk), lambda b, h: (h, 0, 0)),
        pl.BlockSpec((1, 1, dk), lambda b, h: (h, 0, 0)),
        pl.BlockSpec((1, 1, dk), lambda b, h: (h, 0, 0)),
        pl.BlockSpec((1, 1, dv), lambda b, h: (h, 0, 0)),
    ]
    args = [proj3, proj3, proj3, proj3, lr3, lw["wa_f"], lw["wa_b"], lw["ba_f"], lw["ba_b"], lw["onorm"]]
    if has_s0:
        in_specs += [st_spec, st_spec]
        args += [s0f, s0b]
    out_specs = [pl.BlockSpec((1, t, dv), lambda b, h: (b, 0, h))]
    out_shape = [jax.ShapeDtypeStruct((bsz, t, nh * dv), _BF16)]
    if emit_state:
        out_specs += [st_spec, st_spec]
        out_shape += [jax.ShapeDtypeStruct((bsz, nh, dk, dv), _F32)] * 2
    outs = pl.pallas_call(
        functools.partial(_gla_kernel, has_s0=has_s0, emit_state=emit_state, n_chunks=t // GLA_CHUNK),
        grid=(bsz, nh),
        in_specs=in_specs,
        out_specs=out_specs,
        out_shape=out_shape,
        scratch_shapes=[
            pltpu.VMEM((t, dk), _F32), pltpu.VMEM((t, dk), _F32),
            pltpu.VMEM((t, dv), _F32), pltpu.VMEM((t, dv), _F32),
            pltpu.VMEM((dv, dk), _F32), pltpu.VMEM((dv, dk), _F32),
        ],
        compiler_params=_params(("parallel", "parallel")),
        name="gla",
    )(*args)
    return outs if emit_state else (outs[0], None, None)


def _dft_tables(n):
    idx = (np.arange(n)[:, None] * np.arange(n)[None, :]) % n
    ang = 2.0 * np.pi * idx / n
    return np.cos(ang) / math.sqrt(n), np.sin(ang) / math.sqrt(n)


def _fnet_kernel(x_ref, cc_ref, ct_ref, o_ref, y_s):
    t = x_ref.shape[1]
    gd = FOURIER_GROUP_DIM
    for g in range(FOURIER_GROUPS):
        xg = x_ref[0, :, g * gd:(g + 1) * gd].astype(_BF16)
        y = _dot(xg, cc_ref[...])
        y_s[0:t, g * gd:(g + 1) * gd] = y[:, :gd].astype(_BF16)
        y_s[t:2 * t, g * gd:(g + 1) * gd] = y[:, gd:].astype(_BF16)
    o_ref[0] = _dot(ct_ref[...], y_s[...]).astype(o_ref.dtype)


def _fnet(proj3, *, fo_blk):
    bsz, t, _ = proj3.shape
    width = FOURIER_GROUPS * FOURIER_GROUP_DIM
    c_c, s_c = _dft_tables(FOURIER_GROUP_DIM)
    c_t, s_t = _dft_tables(t)
    cc = jnp.asarray(np.concatenate([c_c, s_c], axis=1), _F32).astype(_BF16)
    ct = jnp.asarray(np.concatenate([c_t, -s_t], axis=1), _F32).astype(_BF16)
    return pl.pallas_call(
        _fnet_kernel,
        grid=(bsz,),
        in_specs=[
            pl.BlockSpec((1, t, width), lambda b: (b, 0, fo_blk)),
            pl.BlockSpec(cc.shape, lambda b: (0, 0)),
            pl.BlockSpec(ct.shape, lambda b: (0, 0)),
        ],
        out_specs=pl.BlockSpec((1, t, width), lambda b: (b, 0, 0)),
        out_shape=jax.ShapeDtypeStruct((bsz, t, width), _BF16),
        scratch_shapes=[pltpu.VMEM((2 * t, width), _BF16)],
        compiler_params=_params(("parallel",)),
        name="fnet",
    )(proj3, cc, ct)


def _merge_kernel(x_ref, mod_ref, gpre_ref, gpost_ref, on_ref, f_ref, wga_ref, wgb_ref, bga_ref, bgb_ref,
                  wbg_ref, wbf_ref, wo_ref, o_ref, h_s, acc_s):
    j = pl.program_id(1)

    @pl.when(j == 0)
    def _():
        h_s[...] = _mod_rmsnorm(x_ref[...], gpre_ref[...], mod_ref[0, 0:1, :], mod_ref[0, 1:2, :]).astype(_BF16)
        acc_s[...] = jnp.zeros_like(acc_s)

    h = h_s[...]
    gate_a = jax.nn.sigmoid(_dot(h, wga_ref[...]) + bga_ref[...])
    gate_b = jax.nn.sigmoid(_dot(h, wgb_ref[...]) + bgb_ref[...])
    merged = gate_a * _dot(on_ref[...], wbg_ref[...]) + gate_b * _dot(f_ref[...], wbf_ref[...])
    acc_s[...] += _dot(merged.astype(_BF16), wo_ref[...])

    @pl.when(j == pl.num_programs(1) - 1)
    def _():
        y = _rms_scale(acc_s[...]) * gpost_ref[...]
        o_ref[...] = x_ref[...] + mod_ref[0, 2:3, :] * y


def _merge(x2, mod, on2, f2, lw, *, tm, tn=512):
    m, d = x2.shape
    kb = on2.shape[1]
    nj = d // tn
    tiles_per_mod = (m // mod.shape[0]) // tm
    row = lambda i, j: (i, 0)
    return pl.pallas_call(
        _merge_kernel,
        grid=(m // tm, nj),
        in_specs=[
            pl.BlockSpec((tm, d), row),
            pl.BlockSpec((1, N_MOD, d), lambda i, j: (i // tiles_per_mod, 0, 0)),
            pl.BlockSpec((1, d), lambda i, j: (0, 0)),
            pl.BlockSpec((1, d), lambda i, j: (0, 0)),
            pl.BlockSpec((tm, kb), row),
            pl.BlockSpec((tm, kb), row),
            pl.BlockSpec((d, tn), lambda i, j: (0, j)),
            pl.BlockSpec((d, tn), lambda i, j: (0, nj + j)),
            pl.BlockSpec((1, tn), lambda i, j: (0, j)),
            pl.BlockSpec((1, tn), lambda i, j: (0, nj + j)),
            pl.BlockSpec((kb, tn), lambda i, j: (0, j)),
            pl.BlockSpec((kb, tn), lambda i, j: (0, j)),
            pl.BlockSpec((tn, d), lambda i, j: (j, 0)),
        ],
        out_specs=pl.BlockSpec((tm, d), row),
        out_shape=jax.ShapeDtypeStruct((m, d), _F32),
        scratch_shapes=[pltpu.VMEM((tm, d), _BF16), pltpu.VMEM((tm, d), _F32)],
        compiler_params=_params(("parallel", "arbitrary")),
        name="merge",
    )(x2, mod, lw["g_pre_mix"], lw["g_post_mix"], on2, f2, lw["w_gate"], lw["w_gate"], lw["b_gate"], lw["b_gate"],
      lw["w_br_gla"], lw["w_br_four"], lw["w_out"])


def _ffn_kernel(x_ref, mod_ref, gpre_ref, gpost_ref, wg_ref, wu_ref, wd_ref, o_ref, h_s, acc_s):
    j = pl.program_id(1)

    @pl.when(j == 0)
    def _():
        h_s[...] = _mod_rmsnorm(x_ref[...], gpre_ref[...], mod_ref[0, 3:4, :], mod_ref[0, 4:5, :]).astype(_BF16)
        acc_s[...] = jnp.zeros_like(acc_s)

    h = h_s[...]
    p = _silu(_dot(h, wg_ref[...])) * _dot(h, wu_ref[...])
    acc_s[...] += _dot(p.astype(_BF16), wd_ref[...])

    @pl.when(j == pl.num_programs(1) - 1)
    def _():
        y = _rms_scale(acc_s[...]) * gpost_ref[...]
        o_ref[...] = x_ref[...] + mod_ref[0, 5:6, :] * y


def _ffn(x2, mod, lw, *, tm, th=512):
    m, d = x2.shape
    hidden = lw["w_ffn_gate"].shape[1]
    tiles_per_mod = (m // mod.shape[0]) // tm
    row = lambda i, j: (i, 0)
    return pl.pallas_call(
        _ffn_kernel,
        grid=(m // tm, hidden // th),
        in_specs=[
            pl.BlockSpec((tm, d), row),
            pl.BlockSpec((1, N_MOD, d), lambda i, j: (i // tiles_per_mod, 0, 0)),
            pl.BlockSpec((1, d), lambda i, j: (0, 0)),
            pl.BlockSpec((1, d), lambda i, j: (0, 0)),
            pl.BlockSpec((d, th), lambda i, j: (0, j)),
            pl.BlockSpec((d, th), lambda i, j: (0, j)),
            pl.BlockSpec((th, d), lambda i, j: (j, 0)),
        ],
        out_specs=pl.BlockSpec((tm, d), row),
        out_shape=jax.ShapeDtypeStruct((m, d), _F32),
        scratch_shapes=[pltpu.VMEM((tm, d), _BF16), pltpu.VMEM((tm, d), _F32)],
        compiler_params=_params(("parallel", "arbitrary")),
        name="ffn",
    )(x2, mod, lw["g_pre_ffn"], lw["g_post_ffn"], lw["w_ffn_gate"], lw["w_ffn_up"], lw["w_ffn_down"])


def _prep_layer_weights(l, norm_pre_mix, norm_post_mix, norm_pre_ffn, norm_post_ffn, w_in, w_a2_fwd, b_a_fwd,
                        w_a2_bwd, b_a_bwd, gla_out_norm, w_br_gla, w_br_four, w_gate, b_gate, w_out,
                        w_ffn_gate, w_ffn_up, w_ffn_down):
    nh, dk, dv, rank = GLA_HEADS, GLA_DK, GLA_DV, GLA_RANK
    d = w_in.shape[1]
    qkvg = 2 * nh * dk + 2 * nh * dv
    wi = w_in[l]
    w_main = jnp.concatenate([wi[:, :qkvg], wi[:, qkvg + 2 * rank:]], axis=1).astype(_BF16)
    w_lr = jnp.pad(wi[:, qkvg:qkvg + 2 * rank], ((0, 0), (0, LANES - 2 * rank))).astype(_BF16)

    def decay_map(w_a2, row0):
        w = w_a2.reshape(rank, nh, dk).transpose(1, 0, 2)
        return jnp.pad(w, ((0, 0), (row0, LANES - rank - row0), (0, 0))).astype(_BF16)

    return dict(
        g_pre_mix=norm_pre_mix[l].reshape(1, d), g_post_mix=norm_post_mix[l].reshape(1, d),
        g_pre_ffn=norm_pre_ffn[l].reshape(1, d), g_post_ffn=norm_post_ffn[l].reshape(1, d),
        w_main=w_main, w_lr=w_lr, fo_blk=qkvg // (FOURIER_GROUPS * FOURIER_GROUP_DIM),
        wa_f=decay_map(w_a2_fwd[l], 0), wa_b=decay_map(w_a2_bwd[l], rank),
        ba_f=b_a_fwd[l].reshape(nh, 1, dk), ba_b=b_a_bwd[l].reshape(nh, 1, dk),
        onorm=gla_out_norm[l].reshape(nh, 1, dv),
        w_br_gla=w_br_gla[l].astype(_BF16), w_br_four=w_br_four[l].astype(_BF16),
        w_gate=w_gate[l].astype(_BF16), b_gate=b_gate[l].reshape(1, -1), w_out=w_out[l].astype(_BF16),
        w_ffn_gate=w_ffn_gate[l].astype(_BF16), w_ffn_up=w_ffn_up[l].astype(_BF16),
        w_ffn_down=w_ffn_down[l].astype(_BF16),
    )


def _token_tile(rows_per_mod):
    return min(512, rows_per_mod)


def _layer(x, mod, s0f, s0b, lw, *, emit_state):
    bsz, t, d = x.shape
    m = bsz * t
    tm = _token_tile(m // mod.shape[0])
    x2 = x.reshape(m, d)
    proj, lr = _in_proj(x2, mod, lw["g_pre_mix"], lw["w_main"], lw["w_lr"], tm=tm)
    proj3 = proj.reshape(bsz, t, -1)
    on, sf, sb = _gla(proj3, lr.reshape(bsz, t, LANES), lw, s0f, s0b, emit_state=emit_state)
    f = _fnet(proj3, fo_blk=lw["fo_blk"])
    x1 = _merge(x2, mod, on.reshape(m, -1), f.reshape(m, -1), lw, tm=tm)
    y = _ffn(x1, mod, lw, tm=tm)
    return y.reshape(bsz, t, d), sf, sb


def kernel(x_prompt, x_sample, state_gla_fwd, state_gla_bwd, c, c_ctx, w_ada, b_ada, norm_pre_mix, norm_post_mix,
           norm_pre_ffn, norm_post_ffn, w_in, w_a2_fwd, b_a_fwd, w_a2_bwd, b_a_bwd, gla_out_norm, w_br_gla,
           w_br_four, w_gate, b_gate, w_out, w_ffn_gate, w_ffn_up, w_ffn_down):
    depth = w_ada.shape[0]
    d = x_prompt.shape[-1]
    n_lat = c.shape[0]
    cond_rows = -(-(1 + n_lat) // 8) * 8
    cond = jnp.concatenate([c_ctx[None, :], c, jnp.zeros((cond_rows - 1 - n_lat, d), _F32)], axis=0)
    xp, xs = x_prompt, x_sample
    new_f, new_b = [], []
    for l in range(depth):
        lw = _prep_layer_weights(l, norm_pre_mix, norm_post_mix, norm_pre_ffn, norm_post_ffn, w_in, w_a2_fwd,
                                 b_a_fwd, w_a2_bwd, b_a_bwd, gla_out_norm, w_br_gla, w_br_four, w_gate, b_gate,
                                 w_out, w_ffn_gate, w_ffn_up, w_ffn_down)
        mod = _ada(cond, w_ada[l], b_ada[l])
        mod_ctx = mod[0:1].reshape(1, N_MOD, d)
        mod_lat = mod[1:1 + n_lat].reshape(n_lat, N_MOD, d)
        xp, s_f, s_b = _layer(xp, mod_ctx, None, None, lw, emit_state=True)
        new_f.append(s_f)
        new_b.append(s_b)
        xs, _, _ = _layer(xs, mod_lat, state_gla_fwd[:, l], state_gla_bwd[:, l], lw, emit_state=False)
    return (xp, xs, jnp.stack(new_f, axis=1), jnp.stack(new_b, axis=1))
```

```python
import functools
import math

import numpy as np
import jax
import jax.numpy as jnp
from jax import lax
from jax.experimental import pallas as pl
from jax.experimental.pallas import tpu as pltpu

_F32 = jnp.float32
_BF16 = jnp.bfloat16

EPS = 1e-6
N_MOD = 6
GLA_HEADS = 4
GLA_DK = 128
GLA_DV = 256
GLA_RANK = 16
GLA_CHUNK = 64
GLA_INV_TAU = 1.0 / 16.0
FOURIER_GROUPS = 4
FOURIER_GROUP_DIM = 256

V7X_VMEM_BYTES = 64 * 1024 * 1024
VMEM_LIMIT_BYTES = 56 * 1024 * 1024
LANES = 128


def _dot(a, b):
    return jnp.dot(a, b, preferred_element_type=_F32)


def _dot_nt(a, b):
    return lax.dot_general(a, b, (((1,), (1,)), ((), ())), preferred_element_type=_F32)


def _dot_tn(a, b):
    return lax.dot_general(a, b, (((0,), (0,)), ((), ())), preferred_element_type=_F32)


def _rms_scale(x):
    return x * lax.rsqrt(jnp.mean(x * x, axis=-1, keepdims=True) + EPS)


def _mod_rmsnorm(x, gain, shift, scale):
    return (_rms_scale(x) * gain) * (1.0 + scale) + shift


def _silu(x):
    return x * jax.nn.sigmoid(x)


def _params(semantics):
    return pltpu.CompilerParams(dimension_semantics=semantics, vmem_limit_bytes=VMEM_LIMIT_BYTES)


def _ada_kernel(c_ref, w_ref, b_ref, o_ref):
    s = _silu(c_ref[...]).astype(_BF16)
    o_ref[...] = _dot(s, w_ref[...].astype(_BF16)) + b_ref[...]


def _ada(cond, w_ada, b_ada, *, tn=1024):
    rows, d = cond.shape
    n = w_ada.shape[1]
    return pl.pallas_call(
        _ada_kernel,
        grid=(n // tn,),
        in_specs=[
            pl.BlockSpec((rows, d), lambda j: (0, 0)),
            pl.BlockSpec((d, tn), lambda j: (0, j)),
            pl.BlockSpec((1, tn), lambda j: (0, j)),
        ],
        out_specs=pl.BlockSpec((rows, tn), lambda j: (0, j)),
        out_shape=jax.ShapeDtypeStruct((rows, n), _F32),
        compiler_params=_params(("parallel",)),
        name="ada",
    )(cond, w_ada, b_ada.reshape(1, n))


def _inproj_kernel(x_ref, mod_ref, g_ref, w_ref, wlr_ref, o32_ref, o16_ref, lr_ref, h_s, *, n_f32):
    j = pl.program_id(1)

    @pl.when(j == 0)
    def _():
        h = _mod_rmsnorm(x_ref[...], g_ref[...], mod_ref[0, 0:1, :], mod_ref[0, 1:2, :]).astype(_BF16)
        h_s[...] = h
        lr_ref[...] = _dot(h, wlr_ref[...])

    @pl.when(j < n_f32)
    def _():
        o32_ref[...] = _dot(h_s[...], w_ref[...])

    @pl.when(j >= n_f32)
    def _():
        o16_ref[...] = _dot(h_s[...], w_ref[...]).astype(o16_ref.dtype)


def _in_proj(x2, mod, gain, w_main, w_lr, *, tm, n32, tn=512):
    m, d = x2.shape
    n = w_main.shape[1]
    n_f32 = n32 // tn
    tiles_per_mod = (m // mod.shape[0]) // tm
    return pl.pallas_call(
        functools.partial(_inproj_kernel, n_f32=n_f32),
        grid=(m // tm, n // tn),
        in_specs=[
            pl.BlockSpec((tm, d), lambda i, j: (i, 0)),
            pl.BlockSpec((1, N_MOD, d), lambda i, j: (i // tiles_per_mod, 0, 0)),
            pl.BlockSpec((1, d), lambda i, j: (0, 0)),
            pl.BlockSpec((d, tn), lambda i, j: (0, j)),
            pl.BlockSpec((d, LANES), lambda i, j: (0, 0)),
        ],
        out_specs=[
            pl.BlockSpec((tm, tn), lambda i, j: (i, jnp.minimum(j, n_f32 - 1))),
            pl.BlockSpec((tm, tn), lambda i, j: (i, jnp.maximum(j - n_f32, 0))),
            pl.BlockSpec((tm, LANES), lambda i, j: (i, 0)),
        ],
        out_shape=[
            jax.ShapeDtypeStruct((m, n32), _F32),
            jax.ShapeDtypeStruct((m, n - n32), _BF16),
            jax.ShapeDtypeStruct((m, LANES), _F32),
        ],
        scratch_shapes=[pltpu.VMEM((tm, d), _BF16)],
        compiler_params=_params(("parallel", "arbitrary")),
        name="in_proj",
    )(x2, mod, gain.reshape(1, d), w_main, w_lr)


def _log_sigmoid(x):
    return jnp.minimum(x, 0.0) - jnp.log1p(jnp.exp(-jnp.abs(x)))


def _chunk_cumsum(x, reverse):
    t = x.shape[0]
    pos = lax.broadcasted_iota(jnp.int32, x.shape, 0) & (GLA_CHUNK - 1)
    s = 1
    while s < GLA_CHUNK:
        if reverse:
            shifted = pltpu.roll(x, t - s, axis=0)
            keep = pos < GLA_CHUNK - s
        else:
            shifted = pltpu.roll(x, s, axis=0)
            keep = pos >= s
        x = x + jnp.where(keep, shifted, 0.0)
        s *= 2
    return x


def _gla_kernel(*refs, has_s0, emit_state, n_chunks):
    it = iter(refs)
    q_ref, k_ref, g_ref, v_ref, lr_ref = (next(it) for _ in range(5))
    waf_ref, wab_ref, baf_ref, bab_ref, onorm_ref = (next(it) for _ in range(5))
    s0f_ref, s0b_ref = (next(it), next(it)) if has_s0 else (None, None)
    o_ref = next(it)
    sf_ref, sb_ref = (next(it), next(it)) if emit_state else (None, None)
    bf_s, bb_s, of_s, ob_s, stf_s, stb_s = (next(it) for _ in range(6))

    c, nh, dk, dv = GLA_CHUNK, GLA_HEADS, GLA_DK, GLA_DV
    lr = lr_ref[0].astype(_BF16)
    bf_s[...] = _chunk_cumsum(_log_sigmoid(_dot(lr, waf_ref[...]) + baf_ref[...]) * GLA_INV_TAU, False)
    bb_s[...] = _chunk_cumsum(_log_sigmoid(_dot(lr, wab_ref[...]) + bab_ref[...]) * GLA_INV_TAU, True)

    for h in range(nh):
        if has_s0:
            stf_s[h] = s0f_ref[0, h].T
            stb_s[h] = s0b_ref[0, h].T
        else:
            stf_s[h] = jnp.zeros((dv, dk), _F32)
            stb_s[h] = jnp.zeros((dv, dk), _F32)

    row = lax.broadcasted_iota(jnp.int32, (c, c), 0)
    col = lax.broadcasted_iota(jnp.int32, (c, c), 1)
    q_scale = GLA_DK ** -0.5

    def one_chunk(r0, h, b_s, st_s, out_s, mask, total_row):
        rows = pl.ds(r0, c)
        kcols = slice(h * dk, (h + 1) * dk)
        vcols = slice(h * dv, (h + 1) * dv)
        b = b_s[rows, kcols]
        total = b[total_row:total_row + 1, :]
        q = q_ref[0, rows, kcols] * q_scale
        k = k_ref[0, rows, kcols]
        v = v_ref[0, rows, vcols]
        qi = (q * jnp.exp(b)).astype(_BF16)
        ki = (k * jnp.exp(-b)).astype(_BF16)
        ko = (k * jnp.exp(total - b)).astype(_BF16)
        att = jnp.where(mask, _dot_nt(qi, ki), 0.0).astype(_BF16)
        st = st_s[h]
        out_s[rows, vcols] = _dot(att, v) + _dot_nt(qi, st.astype(_BF16))
        st_s[h] = st * jnp.exp(total) + _dot_tn(v, ko)

    def body(n, carry):
        r_fwd = pl.multiple_of(n * c, c)
        r_bwd = pl.multiple_of((n_chunks - 1 - n) * c, c)
        for h in range(nh):
            one_chunk(r_fwd, h, bf_s, stf_s, of_s, col <= row, c - 1)
            one_chunk(r_bwd, h, bb_s, stb_s, ob_s, col >= row, 0)
        return carry

    lax.fori_loop(0, n_chunks, body, 0)

    for h in range(nh):
        vcols = slice(h * dv, (h + 1) * dv)
        o = of_s[:, vcols] + ob_s[:, vcols]
        o = _rms_scale(o) * onorm_ref[:, vcols]
        o_ref[0, :, vcols] = (o * _silu(g_ref[0, :, vcols])).astype(o_ref.dtype)
        if emit_state:
            sf_ref[0, h] = stf_s[h].T
            sb_ref[0, h] = stb_s[h].T


def _gla(a32, b16, lr3, lw, s0f, s0b, *, emit_state):
    bsz, t, _ = a32.shape
    nh, dk, dv = GLA_HEADS, GLA_DK, GLA_DV
    kw, vw = nh * dk, nh * dv
    has_s0 = s0f is not None
    st_spec = pl.BlockSpec((1, nh, dk, dv), lambda b: (b, 0, 0, 0))
    in_specs = [
        pl.BlockSpec((1, t, kw), lambda b: (b, 0, 0)),
        pl.BlockSpec((1, t, kw), lambda b: (b, 0, 1)),
        pl.BlockSpec((1, t, vw), lambda b: (b, 0, 1)),
        pl.BlockSpec((1, t, vw), lambda b: (b, 0, 0)),
        pl.BlockSpec((1, t, LANES), lambda b: (b, 0, 0)),
        pl.BlockSpec((LANES, kw), lambda b: (0, 0)),
        pl.BlockSpec((LANES, kw), lambda b: (0, 0)),
        pl.BlockSpec((1, kw), lambda b: (0, 0)),
        pl.BlockSpec((1, kw), lambda b: (0, 0)),
        pl.BlockSpec((1, vw), lambda b: (0, 0)),
    ]
    args = [a32, a32, a32, b16, lr3, lw["wa_f"], lw["wa_b"], lw["ba_f"], lw["ba_b"], lw["onorm"]]
    if has_s0:
        in_specs += [st_spec, st_spec]
        args += [s0f, s0b]
    out_specs = [pl.BlockSpec((1, t, vw), lambda b: (b, 0, 0))]
    out_shape = [jax.ShapeDtypeStruct((bsz, t, vw), _BF16)]
    if emit_state:
        out_specs += [st_spec, st_spec]
        out_shape += [jax.ShapeDtypeStruct((bsz, nh, dk, dv), _F32)] * 2
    outs = pl.pallas_call(
        functools.partial(_gla_kernel, has_s0=has_s0, emit_state=emit_state, n_chunks=t // GLA_CHUNK),
        grid=(bsz,),
        in_specs=in_specs,
        out_specs=out_specs,
        out_shape=out_shape,
        scratch_shapes=[
            pltpu.VMEM((t, kw), _F32), pltpu.VMEM((t, kw), _F32),
            pltpu.VMEM((t, vw), _F32), pltpu.VMEM((t, vw), _F32),
            pltpu.VMEM((nh, dv, dk), _F32), pltpu.VMEM((nh, dv, dk), _F32),
        ],
        compiler_params=_params(("parallel",)),
        name="gla",
    )(*args)
    return outs if emit_state else (outs[0], None, None)


def _dft_tables(n):
    idx = (np.arange(n)[:, None] * np.arange(n)[None, :]) % n
    ang = 2.0 * np.pi * idx / n
    return np.cos(ang) / math.sqrt(n), np.sin(ang) / math.sqrt(n)


def _fnet_kernel(x_ref, cc_ref, ct_ref, o_ref, y_s):
    t = x_ref.shape[1]
    gd = FOURIER_GROUP_DIM
    for g in range(FOURIER_GROUPS):
        y = _dot(x_ref[0, :, g * gd:(g + 1) * gd], cc_ref[...])
        y_s[0:t, g * gd:(g + 1) * gd] = y[:, :gd].astype(_BF16)
        y_s[t:2 * t, g * gd:(g + 1) * gd] = y[:, gd:].astype(_BF16)
    o_ref[0] = _dot(ct_ref[...], y_s[...]).astype(o_ref.dtype)


def _fnet(b16, *, fo_blk):
    bsz, t, _ = b16.shape
    width = FOURIER_GROUPS * FOURIER_GROUP_DIM
    c_c, s_c = _dft_tables(FOURIER_GROUP_DIM)
    c_t, s_t = _dft_tables(t)
    cc = jnp.asarray(np.concatenate([c_c, s_c], axis=1), _F32).astype(_BF16)
    ct = jnp.asarray(np.concatenate([c_t, -s_t], axis=1), _F32).astype(_BF16)
    return pl.pallas_call(
        _fnet_kernel,
        grid=(bsz,),
        in_specs=[
            pl.BlockSpec((1, t, width), lambda b: (b, 0, fo_blk)),
            pl.BlockSpec(cc.shape, lambda b: (0, 0)),
            pl.BlockSpec(ct.shape, lambda b: (0, 0)),
        ],
        out_specs=pl.BlockSpec((1, t, width), lambda b: (b, 0, 0)),
        out_shape=jax.ShapeDtypeStruct((bsz, t, width), _BF16),
        scratch_shapes=[pltpu.VMEM((2 * t, width), _BF16)],
        compiler_params=_params(("parallel",)),
        name="fnet",
    )(b16, cc, ct)


def _merge_kernel(x_ref, mod_ref, gpre_ref, gpost_ref, on_ref, f_ref, wga_ref, wgb_ref, bga_ref, bgb_ref,
                  wbg_ref, wbf_ref, wo_ref, o_ref, h_s, acc_s):
    j = pl.program_id(1)

    @pl.when(j == 0)
    def _():
        h_s[...] = _mod_rmsnorm(x_ref[...], gpre_ref[...], mod_ref[0, 0:1, :], mod_ref[0, 1:2, :]).astype(_BF16)
        acc_s[...] = jnp.zeros_like(acc_s)

    h = h_s[...]
    gate_a = jax.nn.sigmoid(_dot(h, wga_ref[...]) + bga_ref[...])
    gate_b = jax.nn.sigmoid(_dot(h, wgb_ref[...]) + bgb_ref[...])
    merged = gate_a * _dot(on_ref[...], wbg_ref[...]) + gate_b * _dot(f_ref[...], wbf_ref[...])
    acc_s[...] += _dot(merged.astype(_BF16), wo_ref[...])

    @pl.when(j == pl.num_programs(1) - 1)
    def _():
        y = _rms_scale(acc_s[...]) * gpost_ref[...]
        o_ref[...] = x_ref[...] + mod_ref[0, 2:3, :] * y


def _merge(x2, mod, on2, f2, lw, *, tm, tn=512):
    m, d = x2.shape
    kb = on2.shape[1]
    nj = d // tn
    tiles_per_mod = (m // mod.shape[0]) // tm
    row = lambda i, j: (i, 0)
    return pl.pallas_call(
        _merge_kernel,
        grid=(m // tm, nj),
        in_specs=[
            pl.BlockSpec((tm, d), row),
            pl.BlockSpec((1, N_MOD, d), lambda i, j: (i // tiles_per_mod, 0, 0)),
            pl.BlockSpec((1, d), lambda i, j: (0, 0)),
            pl.BlockSpec((1, d), lambda i, j: (0, 0)),
            pl.BlockSpec((tm, kb), row),
            pl.BlockSpec((tm, kb), row),
            pl.BlockSpec((d, tn), lambda i, j: (0, j)),
            pl.BlockSpec((d, tn), lambda i, j: (0, nj + j)),
            pl.BlockSpec((1, tn), lambda i, j: (0, j)),
            pl.BlockSpec((1, tn), lambda i, j: (0, nj + j)),
            pl.BlockSpec((kb, tn), lambda i, j: (0, j)),
            pl.BlockSpec((kb, tn), lambda i, j: (0, j)),
            pl.BlockSpec((tn, d), lambda i, j: (j, 0)),
        ],
        out_specs=pl.BlockSpec((tm, d), row),
        out_shape=jax.ShapeDtypeStruct((m, d), _F32),
        scratch_shapes=[pltpu.VMEM((tm, d), _BF16), pltpu.VMEM((tm, d), _F32)],
        compiler_params=_params(("parallel", "arbitrary")),
        name="merge",
    )(x2, mod, lw["g_pre_mix"], lw["g_post_mix"], on2, f2, lw["w_gate"], lw["w_gate"], lw["b_gate"], lw["b_gate"],
      lw["w_br_gla"], lw["w_br_four"], lw["w_out"])


def _ffn_kernel(x_ref, mod_ref, gpre_ref, gpost_ref, wg_ref, wu_ref, wd_ref, o_ref, h_s, acc_s):
    j = pl.program_id(1)

    @pl.when(j == 0)
    def _():
        h_s[...] = _mod_rmsnorm(x_ref[...], gpre_ref[...], mod_ref[0, 3:4, :], mod_ref[0, 4:5, :]).astype(_BF16)
        acc_s[...] = jnp.zeros_like(acc_s)

    h = h_s[...]
    p = _silu(_dot(h, wg_ref[...])) * _dot(h, wu_ref[...])
    acc_s[...] += _dot(p.astype(_BF16), wd_ref[...])

    @pl.when(j == pl.num_programs(1) - 1)
    def _():
        y = _rms_scale(acc_s[...]) * gpost_ref[...]
        o_ref[...] = x_ref[...] + mod_ref[0, 5:6, :] * y


def _ffn(x2, mod, lw, *, tm, th=512):
    m, d = x2.shape
    hidden = lw["w_ffn_gate"].shape[1]
    tiles_per_mod = (m // mod.shape[0]) // tm
    row = lambda i, j: (i, 0)
    return pl.pallas_call(
        _ffn_kernel,
        grid=(m // tm, hidden // th),
        in_specs=[
            pl.BlockSpec((tm, d), row),
            pl.BlockSpec((1, N_MOD, d), lambda i, j: (i // tiles_per_mod, 0, 0)),
            pl.BlockSpec((1, d), lambda i, j: (0, 0)),
            pl.BlockSpec((1, d), lambda i, j: (0, 0)),
            pl.BlockSpec((d, th), lambda i, j: (0, j)),
            pl.BlockSpec((d, th), lambda i, j: (0, j)),
            pl.BlockSpec((th, d), lambda i, j: (j, 0)),
        ],
        out_specs=pl.BlockSpec((tm, d), row),
        out_shape=jax.ShapeDtypeStruct((m, d), _F32),
        scratch_shapes=[pltpu.VMEM((tm, d), _BF16), pltpu.VMEM((tm, d), _F32)],
        compiler_params=_params(("parallel", "arbitrary")),
        name="ffn",
    )(x2, mod, lw["g_pre_ffn"], lw["g_post_ffn"], lw["w_ffn_gate"], lw["w_ffn_up"], lw["w_ffn_down"])


def _prep_layer_weights(l, norm_pre_mix, norm_post_mix, norm_pre_ffn, norm_post_ffn, w_in, w_a2_fwd, b_a_fwd,
                        w_a2_bwd, b_a_bwd, gla_out_norm, w_br_gla, w_br_four, w_gate, b_gate, w_out,
                        w_ffn_gate, w_ffn_up, w_ffn_down):
    nh, dk, dv, rank = GLA_HEADS, GLA_DK, GLA_DV, GLA_RANK
    d = w_in.shape[1]
    kw, vw = nh * dk, nh * dv
    wi = w_in[l]
    q_k = wi[:, :2 * kw]
    v = wi[:, 2 * kw:2 * kw + vw]
    g = wi[:, 2 * kw + vw:2 * kw + 2 * vw]
    lr0 = 2 * kw + 2 * vw
    fo = wi[:, lr0 + 2 * rank:]
    w_main = jnp.concatenate([q_k, g, v, fo], axis=1).astype(_BF16)
    w_lr = jnp.pad(wi[:, lr0:lr0 + 2 * rank], ((0, 0), (0, LANES - 2 * rank))).astype(_BF16)

    def decay_map(w_a2, row0):
        return jnp.pad(w_a2, ((row0, LANES - rank - row0), (0, 0))).astype(_BF16)

    return dict(
        g_pre_mix=norm_pre_mix[l].reshape(1, d), g_post_mix=norm_post_mix[l].reshape(1, d),
        g_pre_ffn=norm_pre_ffn[l].reshape(1, d), g_post_ffn=norm_post_ffn[l].reshape(1, d),
        w_main=w_main, w_lr=w_lr, n32=2 * kw + vw, fo_blk=vw // (FOURIER_GROUPS * FOURIER_GROUP_DIM),
        wa_f=decay_map(w_a2_fwd[l], 0), wa_b=decay_map(w_a2_bwd[l], rank),
        ba_f=b_a_fwd[l].reshape(1, kw), ba_b=b_a_bwd[l].reshape(1, kw),
        onorm=gla_out_norm[l].reshape(1, vw),
        w_br_gla=w_br_gla[l].astype(_BF16), w_br_four=w_br_four[l].astype(_BF16),
        w_gate=w_gate[l].astype(_BF16), b_gate=b_gate[l].reshape(1, -1), w_out=w_out[l].astype(_BF16),
        w_ffn_gate=w_ffn_gate[l].astype(_BF16), w_ffn_up=w_ffn_up[l].astype(_BF16),
        w_ffn_down=w_ffn_down[l].astype(_BF16),
    )


def _layer(x, mod, s0f, s0b, lw, *, emit_state):
    bsz, t, d = x.shape
    m = bsz * t
    rows_per_mod = m // mod.shape[0]
    tm = min(512, rows_per_mod)
    x2 = x.reshape(m, d)
    a32, b16, lr = _in_proj(x2, mod, lw["g_pre_mix"], lw["w_main"], lw["w_lr"], tm=min(1024, rows_per_mod),
                            n32=lw["n32"])
    b16 = b16.reshape(bsz, t, -1)
    on, sf, sb = _gla(a32.reshape(bsz, t, -1), b16, lr.reshape(bsz, t, LANES), lw, s0f, s0b, emit_state=emit_state)
    f = _fnet(b16, fo_blk=lw["fo_blk"])
    x1 = _merge(x2, mod, on.reshape(m, -1), f.reshape(m, -1), lw, tm=tm)
    y = _ffn(x1, mod, lw, tm=tm)
    return y.reshape(bsz, t, d), sf, sb


def kernel(x_prompt, x_sample, state_gla_fwd, state_gla_bwd, c, c_ctx, w_ada, b_ada, norm_pre_mix, norm_post_mix,
           norm_pre_ffn, norm_post_ffn, w_in, w_a2_fwd, b_a_fwd, w_a2_bwd, b_a_bwd, gla_out_norm, w_br_gla,
           w_br_four, w_gate, b_gate, w_out, w_ffn_gate, w_ffn_up, w_ffn_down):
    depth = w_ada.shape[0]
    d = x_prompt.shape[-1]
    n_lat = c.shape[0]
    cond_rows = -(-(1 + n_lat) // 8) * 8
    cond = jnp.concatenate([c_ctx[None, :], c, jnp.zeros((cond_rows - 1 - n_lat, d), _F32)], axis=0)
    xp, xs = x_prompt, x_sample
    new_f, new_b = [], []
    for l in range(depth):
        lw = _prep_layer_weights(l, norm_pre_mix, norm_post_mix, norm_pre_ffn, norm_post_ffn, w_in, w_a2_fwd,
                                 b_a_fwd, w_a2_bwd, b_a_bwd, gla_out_norm, w_br_gla, w_br_four, w_gate, b_gate,
                                 w_out, w_ffn_gate, w_ffn_up, w_ffn_down)
        mod = _ada(cond, w_ada[l], b_ada[l])
        mod_ctx = mod[0:1].reshape(1, N_MOD, d)
        mod_lat = mod[1:1 + n_lat].reshape(n_lat, N_MOD, d)
        xp, s_f, s_b = _layer(xp, mod_ctx, None, None, lw, emit_state=True)
        new_f.append(s_f)
        new_b.append(s_b)
        xs, _, _ = _layer(xs, mod_lat, state_gla_fwd[:, l], state_gla_bwd[:, l], lw, emit_state=False)
    return (xp, xs, jnp.stack(new_f, axis=1), jnp.stack(new_b, axis=1))
```

```python
import functools
import math

import numpy as np
import jax
import jax.numpy as jnp
from jax import lax
from jax.experimental import pallas as pl
from jax.experimental.pallas import tpu as pltpu

_F32 = jnp.float32
_BF16 = jnp.bfloat16

EPS = 1e-6
N_MOD = 6
GLA_HEADS = 4
GLA_DK = 128
GLA_DV = 256
GLA_RANK = 16
GLA_CHUNK = 64
GLA_INV_TAU = 1.0 / 16.0
FOURIER_GROUPS = 4
FOURIER_GROUP_DIM = 256

V7X_VMEM_BYTES = 64 * 1024 * 1024
VMEM_LIMIT_BYTES = 56 * 1024 * 1024
LANES = 128


def _dot(a, b):
    return jnp.dot(a, b, preferred_element_type=_F32)


def _dot_nt(a, b):
    return lax.dot_general(a, b, (((1,), (1,)), ((), ())), preferred_element_type=_F32)


def _dot_tn(a, b):
    return lax.dot_general(a, b, (((0,), (0,)), ((), ())), preferred_element_type=_F32)


def _rms_scale(x):
    return x * lax.rsqrt(jnp.mean(x * x, axis=-1, keepdims=True) + EPS)


def _mod_rmsnorm(x, gain, shift, scale):
    return (_rms_scale(x) * gain) * (1.0 + scale) + shift


def _silu(x):
    return x * jax.nn.sigmoid(x)


def _params(semantics):
    return pltpu.CompilerParams(dimension_semantics=semantics, vmem_limit_bytes=VMEM_LIMIT_BYTES)


def _ada_kernel(c_ref, w_ref, b_ref, o_ref):
    s = _silu(c_ref[...]).astype(_BF16)
    o_ref[...] = _dot(s, w_ref[...].astype(_BF16)) + b_ref[...]


def _ada(cond, w_ada, b_ada, *, tn=1024):
    rows, d = cond.shape
    n = w_ada.shape[1]
    return pl.pallas_call(
        _ada_kernel,
        grid=(n // tn,),
        in_specs=[
            pl.BlockSpec((rows, d), lambda j: (0, 0)),
            pl.BlockSpec((d, tn), lambda j: (0, j)),
            pl.BlockSpec((1, tn), lambda j: (0, j)),
        ],
        out_specs=pl.BlockSpec((rows, tn), lambda j: (0, j)),
        out_shape=jax.ShapeDtypeStruct((rows, n), _F32),
        compiler_params=_params(("parallel",)),
        name="ada",
    )(cond, w_ada, b_ada.reshape(1, n))


def _row_chunk(j, rows_per_step, tm):
    return pl.ds(pl.multiple_of(jnp.minimum(j * rows_per_step, tm - rows_per_step), 8), rows_per_step)


def _rows_per_step(tm, n_steps):
    return min(tm, -(-tm // (8 * n_steps)) * 8)


def _inproj_kernel(x_ref, mod_ref, g_ref, w_ref, wlr_ref, o32_ref, o16_ref, lr_ref, h0, h1, *, n_f32, rows_per_step):
    s, j = pl.program_id(0), pl.program_id(1)
    tm = x_ref.shape[0]
    rows = _row_chunk(j, rows_per_step, tm)

    def normalise_rows(h_fill):
        h_fill[rows, :] = _mod_rmsnorm(x_ref[rows, :], g_ref[...], mod_ref[0, 0:1, :],
                                       mod_ref[0, 1:2, :]).astype(_BF16)

    @pl.when(s == 0)
    def _():
        normalise_rows(h0)

    for parity, (h_fill, h_drain) in enumerate(((h0, h1), (h1, h0))):
        live = (s > 0) & (s % 2 == parity)

        @pl.when(live & (j == 0))
        def _(h_drain=h_drain):
            lr_ref[...] = _dot(h_drain[...], wlr_ref[...])

        @pl.when(live & (j < n_f32))
        def _(h_fill=h_fill, h_drain=h_drain):
            normalise_rows(h_fill)
            o32_ref[...] = _dot(h_drain[...], w_ref[...])

        @pl.when(live & (j >= n_f32))
        def _(h_fill=h_fill, h_drain=h_drain):
            normalise_rows(h_fill)
            o16_ref[...] = _dot(h_drain[...], w_ref[...]).astype(o16_ref.dtype)


def _in_proj(x2, mod, gain, w_main, w_lr, *, tm, n32, tn=512):
    m, d = x2.shape
    n = w_main.shape[1]
    n_f32 = n32 // tn
    nj = n // tn
    n_tiles = m // tm
    tiles_per_mod = (m // mod.shape[0]) // tm
    fill_tile = lambda s: jnp.minimum(s, n_tiles - 1)
    drain_tile = lambda s: jnp.maximum(s - 1, 0)
    return pl.pallas_call(
        functools.partial(_inproj_kernel, n_f32=n_f32, rows_per_step=_rows_per_step(tm, nj)),
        grid=(n_tiles + 1, nj),
        in_specs=[
            pl.BlockSpec((tm, d), lambda s, j: (fill_tile(s), 0)),
            pl.BlockSpec((1, N_MOD, d), lambda s, j: (fill_tile(s) // tiles_per_mod, 0, 0)),
            pl.BlockSpec((1, d), lambda s, j: (0, 0)),
            pl.BlockSpec((d, tn), lambda s, j: (0, jnp.where(s > 0, j, 0))),
            pl.BlockSpec((d, LANES), lambda s, j: (0, 0)),
        ],
        out_specs=[
            pl.BlockSpec((tm, tn), lambda s, j: (drain_tile(s), jnp.where(s > 0, jnp.minimum(j, n_f32 - 1), 0))),
            pl.BlockSpec((tm, tn), lambda s, j: (drain_tile(s), jnp.where(s > 0, jnp.maximum(j - n_f32, 0), 0))),
            pl.BlockSpec((tm, LANES), lambda s, j: (drain_tile(s), 0)),
        ],
        out_shape=[
            jax.ShapeDtypeStruct((m, n32), _F32),
            jax.ShapeDtypeStruct((m, n - n32), _BF16),
            jax.ShapeDtypeStruct((m, LANES), _F32),
        ],
        scratch_shapes=[pltpu.VMEM((tm, d), _BF16)] * 2,
        compiler_params=_params(("arbitrary", "arbitrary")),
        name="in_proj",
    )(x2, mod, gain.reshape(1, d), w_main, w_lr)


def _log_sigmoid(x):
    return jnp.minimum(x, 0.0) - jnp.log1p(jnp.exp(-jnp.abs(x)))


def _chunk_cumsum(x, reverse):
    t = x.shape[0]
    pos = lax.broadcasted_iota(jnp.int32, x.shape, 0) & (GLA_CHUNK - 1)
    s = 1
    while s < GLA_CHUNK:
        if reverse:
            shifted = pltpu.roll(x, t - s, axis=0)
            keep = pos < GLA_CHUNK - s
        else:
            shifted = pltpu.roll(x, s, axis=0)
            keep = pos >= s
        x = x + jnp.where(keep, shifted, 0.0)
        s *= 2
    return x


def _gla_kernel(*refs, has_s0, emit_state, n_chunks):
    it = iter(refs)
    q_ref, k_ref, g_ref, v_ref, lr_ref = (next(it) for _ in range(5))
    waf_ref, wab_ref, baf_ref, bab_ref, onorm_ref = (next(it) for _ in range(5))
    s0f_ref, s0b_ref = (next(it), next(it)) if has_s0 else (None, None)
    o_ref = next(it)
    sf_ref, sb_ref = (next(it), next(it)) if emit_state else (None, None)
    bf_s, bb_s, of_s, ob_s, stf_s, stb_s = (next(it) for _ in range(6))

    c, nh, dk, dv = GLA_CHUNK, GLA_HEADS, GLA_DK, GLA_DV
    lr = lr_ref[0].astype(_BF16)
    bf_s[...] = _chunk_cumsum(_log_sigmoid(_dot(lr, waf_ref[...]) + baf_ref[...]) * GLA_INV_TAU, False)
    bb_s[...] = _chunk_cumsum(_log_sigmoid(_dot(lr, wab_ref[...]) + bab_ref[...]) * GLA_INV_TAU, True)

    for h in range(nh):
        if has_s0:
            stf_s[h] = s0f_ref[0, h].T
            stb_s[h] = s0b_ref[0, h].T
        else:
            stf_s[h] = jnp.zeros((dv, dk), _F32)
            stb_s[h] = jnp.zeros((dv, dk), _F32)

    row = lax.broadcasted_iota(jnp.int32, (c, c), 0)
    col = lax.broadcasted_iota(jnp.int32, (c, c), 1)
    q_scale = GLA_DK ** -0.5

    def one_chunk(r0, h, b_s, st_s, out_s, mask, total_row):
        rows = pl.ds(r0, c)
        kcols = slice(h * dk, (h + 1) * dk)
        vcols = slice(h * dv, (h + 1) * dv)
        b = b_s[rows, kcols]
        total = b[total_row:total_row + 1, :]
        q = q_ref[0, rows, kcols] * q_scale
        k = k_ref[0, rows, kcols]
        v = v_ref[0, rows, vcols]
        qi = (q * jnp.exp(b)).astype(_BF16)
        ki = (k * jnp.exp(-b)).astype(_BF16)
        ko = (k * jnp.exp(total - b)).astype(_BF16)
        att = jnp.where(mask, _dot_nt(qi, ki), 0.0).astype(_BF16)
        st = st_s[h]
        out_s[rows, vcols] = _dot(att, v) + _dot_nt(qi, st.astype(_BF16))
        st_s[h] = st * jnp.exp(total) + _dot_tn(v, ko)

    def body(n, carry):
        r_fwd = pl.multiple_of(n * c, c)
        r_bwd = pl.multiple_of((n_chunks - 1 - n) * c, c)
        for h in range(nh):
            one_chunk(r_fwd, h, bf_s, stf_s, of_s, col <= row, c - 1)
            one_chunk(r_bwd, h, bb_s, stb_s, ob_s, col >= row, 0)
        return carry

    lax.fori_loop(0, n_chunks, body, 0)

    for h in range(nh):
        vcols = slice(h * dv, (h + 1) * dv)
        o = of_s[:, vcols] + ob_s[:, vcols]
        o = _rms_scale(o) * onorm_ref[:, vcols]
        o_ref[0, :, vcols] = (o * _silu(g_ref[0, :, vcols])).astype(o_ref.dtype)
        if emit_state:
            sf_ref[0, h] = stf_s[h].T
            sb_ref[0, h] = stb_s[h].T


def _gla(a32, b16, lr3, lw, s0f, s0b, *, emit_state):
    bsz, t, _ = a32.shape
    nh, dk, dv = GLA_HEADS, GLA_DK, GLA_DV
    kw, vw = nh * dk, nh * dv
    has_s0 = s0f is not None
    st_spec = pl.BlockSpec((1, nh, dk, dv), lambda b: (b, 0, 0, 0))
    in_specs = [
        pl.BlockSpec((1, t, kw), lambda b: (b, 0, 0)),
        pl.BlockSpec((1, t, kw), lambda b: (b, 0, 1)),
        pl.BlockSpec((1, t, vw), lambda b: (b, 0, 1)),
        pl.BlockSpec((1, t, vw), lambda b: (b, 0, 0)),
        pl.BlockSpec((1, t, LANES), lambda b: (b, 0, 0)),
        pl.BlockSpec((LANES, kw), lambda b: (0, 0)),
        pl.BlockSpec((LANES, kw), lambda b: (0, 0)),
        pl.BlockSpec((1, kw), lambda b: (0, 0)),
        pl.BlockSpec((1, kw), lambda b: (0, 0)),
        pl.BlockSpec((1, vw), lambda b: (0, 0)),
    ]
    args = [a32, a32, a32, b16, lr3, lw["wa_f"], lw["wa_b"], lw["ba_f"], lw["ba_b"], lw["onorm"]]
    if has_s0:
        in_specs += [st_spec, st_spec]
        args += [s0f, s0b]
    out_specs = [pl.BlockSpec((1, t, vw), lambda b: (b, 0, 0))]
    out_shape = [jax.ShapeDtypeStruct((bsz, t, vw), _BF16)]
    if emit_state:
        out_specs += [st_spec, st_spec]
        out_shape += [jax.ShapeDtypeStruct((bsz, nh, dk, dv), _F32)] * 2
    outs = pl.pallas_call(
        functools.partial(_gla_kernel, has_s0=has_s0, emit_state=emit_state, n_chunks=t // GLA_CHUNK),
        grid=(bsz,),
        in_specs=in_specs,
        out_specs=out_specs,
        out_shape=out_shape,
        scratch_shapes=[
            pltpu.VMEM((t, kw), _F32), pltpu.VMEM((t, kw), _F32),
            pltpu.VMEM((t, vw), _F32), pltpu.VMEM((t, vw), _F32),
            pltpu.VMEM((nh, dv, dk), _F32), pltpu.VMEM((nh, dv, dk), _F32),
        ],
        compiler_params=_params(("parallel",)),
        name="gla",
    )(*args)
    return outs if emit_state else (outs[0], None, None)


def _dft_tables(n):
    idx = (np.arange(n)[:, None] * np.arange(n)[None, :]) % n
    ang = 2.0 * np.pi * idx / n
    return np.cos(ang) / math.sqrt(n), np.sin(ang) / math.sqrt(n)


def _fnet_kernel(x_ref, cc_ref, ct_ref, o_ref, y_s):
    t = x_ref.shape[1]
    gd = FOURIER_GROUP_DIM
    for g in range(FOURIER_GROUPS):
        y = _dot(x_ref[0, :, g * gd:(g + 1) * gd], cc_ref[...])
        y_s[0:t, g * gd:(g + 1) * gd] = y[:, :gd].astype(_BF16)
        y_s[t:2 * t, g * gd:(g + 1) * gd] = y[:, gd:].astype(_BF16)
    o_ref[0] = _dot(ct_ref[...], y_s[...]).astype(o_ref.dtype)


def _fnet(b16, *, fo_blk):
    bsz, t, _ = b16.shape
    width = FOURIER_GROUPS * FOURIER_GROUP_DIM
    c_c, s_c = _dft_tables(FOURIER_GROUP_DIM)
    c_t, s_t = _dft_tables(t)
    cc = jnp.asarray(np.concatenate([c_c, s_c], axis=1), _F32).astype(_BF16)
    ct = jnp.asarray(np.concatenate([c_t, -s_t], axis=1), _F32).astype(_BF16)
    return pl.pallas_call(
        _fnet_kernel,
        grid=(bsz,),
        in_specs=[
            pl.BlockSpec((1, t, width), lambda b: (b, 0, fo_blk)),
            pl.BlockSpec(cc.shape, lambda b: (0, 0)),
            pl.BlockSpec(ct.shape, lambda b: (0, 0)),
        ],
        out_specs=pl.BlockSpec((1, t, width), lambda b: (b, 0, 0)),
        out_shape=jax.ShapeDtypeStruct((bsz, t, width), _BF16),
        scratch_shapes=[pltpu.VMEM((2 * t, width), _BF16)],
        compiler_params=_params(("parallel",)),
        name="fnet",
    )(b16, cc, ct)


class _Skew:
    def __init__(self, n_tiles, nj, tiles_per_mod):
        self.n_tiles, self.nj, self.tiles_per_mod = n_tiles, nj, tiles_per_mod

    def nxt(self, s):
        return jnp.minimum(s + 1, self.n_tiles - 1)

    def cur(self, s):
        return jnp.minimum(s, self.n_tiles - 1)

    def prev(self, s):
        return jnp.maximum(s - 1, 0)

    def col(self, s, j):
        return jnp.where(s < self.n_tiles, j, self.nj - 1)

    def tile_spec(self, shape, which):
        return pl.BlockSpec(shape, lambda s, j: (which(s), 0))

    def mod_spec(self, d, which):
        return pl.BlockSpec((1, N_MOD, d), lambda s, j: (which(s) // self.tiles_per_mod, 0, 0))


def _skewed_sublayer(s, j, n_tiles, rows_per_step, x_next_ref, x_prev_ref, mod_next_ref, mod_prev_ref, gpre_ref,
                     gpost_ref, o_ref, h_s, acc_s, mod_row0, matmul_step):
    tm = x_next_ref.shape[0]
    rows = _row_chunk(j, rows_per_step, tm)

    def normalise(x, mod_ref):
        return _mod_rmsnorm(x, gpre_ref[...], mod_ref[0, mod_row0:mod_row0 + 1, :],
                            mod_ref[0, mod_row0 + 1:mod_row0 + 2, :]).astype(_BF16)

    def finish_prev_rows(acc_prev):
        y = _rms_scale(acc_prev[rows, :]) * gpost_ref[...]
        o_ref[rows, :] = x_prev_ref[rows, :] + mod_prev_ref[0, mod_row0 + 2:mod_row0 + 3, :] * y

    @pl.when((s == 0) & (j == 0))
    def _():
        h_s[0][...] = normalise(x_prev_ref[...], mod_prev_ref)
        for acc in acc_s:
            acc[...] = jnp.zeros_like(acc)

    for parity in range(2):
        h_cur, h_next = h_s[parity], h_s[1 - parity]
        acc_cur, acc_prev = acc_s[parity], acc_s[1 - parity]

        @pl.when((s < n_tiles) & (s % 2 == parity))
        def _(h_cur=h_cur, h_next=h_next, acc_cur=acc_cur, acc_prev=acc_prev):
            h_next[rows, :] = normalise(x_next_ref[rows, :], mod_next_ref)
            finish_prev_rows(acc_prev)
            acc_cur[...] = jnp.where(j > 0, acc_cur[...], 0.0) + matmul_step(h_cur[...])

    @pl.when(s == n_tiles)
    def _():
        finish_prev_rows(acc_s[(n_tiles - 1) % 2])


def _merge_kernel(xn_ref, xp_ref, modn_ref, modp_ref, gpre_ref, gpost_ref, on_ref, f_ref, wga_ref, wgb_ref,
                  bga_ref, bgb_ref, wbg_ref, wbf_ref, wo_ref, o_ref, h0, h1, acc0, acc1, *, n_tiles, rows_per_step):
    def matmul_step(h):
        gate_a = jax.nn.sigmoid(_dot(h, wga_ref[...]) + bga_ref[...])
        gate_b = jax.nn.sigmoid(_dot(h, wgb_ref[...]) + bgb_ref[...])
        merged = gate_a * _dot(on_ref[...], wbg_ref[...]) + gate_b * _dot(f_ref[...], wbf_ref[...])
        return _dot(merged.astype(_BF16), wo_ref[...])

    _skewed_sublayer(pl.program_id(0), pl.program_id(1), n_tiles, rows_per_step, xn_ref, xp_ref, modn_ref,
                     modp_ref, gpre_ref, gpost_ref, o_ref, (h0, h1), (acc0, acc1), 0, matmul_step)


def _merge(x2, mod, on2, f2, lw, *, tm, tn=256):
    m, d = x2.shape
    kb = on2.shape[1]
    nj = d // tn
    n_tiles = m // tm
    sk = _Skew(n_tiles, nj, (m // mod.shape[0]) // tm)
    return pl.pallas_call(
        functools.partial(_merge_kernel, n_tiles=n_tiles, rows_per_step=_rows_per_step(tm, nj)),
        grid=(n_tiles + 1, nj),
        in_specs=[
            sk.tile_spec((tm, d), sk.nxt),
            sk.tile_spec((tm, d), sk.prev),
            sk.mod_spec(d, sk.nxt),
            sk.mod_spec(d, sk.prev),
            pl.BlockSpec((1, d), lambda s, j: (0, 0)),
            pl.BlockSpec((1, d), lambda s, j: (0, 0)),
            sk.tile_spec((tm, kb), sk.cur),
            sk.tile_spec((tm, kb), sk.cur),
            pl.BlockSpec((d, tn), lambda s, j: (0, sk.col(s, j))),
            pl.BlockSpec((d, tn), lambda s, j: (0, nj + sk.col(s, j))),
            pl.BlockSpec((1, tn), lambda s, j: (0, sk.col(s, j))),
            pl.BlockSpec((1, tn), lambda s, j: (0, nj + sk.col(s, j))),
            pl.BlockSpec((kb, tn), lambda s, j: (0, sk.col(s, j))),
            pl.BlockSpec((kb, tn), lambda s, j: (0, sk.col(s, j))),
            pl.BlockSpec((tn, d), lambda s, j: (sk.col(s, j), 0)),
        ],
        out_specs=sk.tile_spec((tm, d), sk.prev),
        out_shape=jax.ShapeDtypeStruct((m, d), _F32),
        scratch_shapes=[pltpu.VMEM((tm, d), _BF16)] * 2 + [pltpu.VMEM((tm, d), _F32)] * 2,
        compiler_params=_params(("arbitrary", "arbitrary")),
        name="merge",
    )(x2, x2, mod, mod, lw["g_pre_mix"], lw["g_post_mix"], on2, f2, lw["w_gate"], lw["w_gate"], lw["b_gate"],
      lw["b_gate"], lw["w_br_gla"], lw["w_br_four"], lw["w_out"])


def _ffn_kernel(xn_ref, xp_ref, modn_ref, modp_ref, gpre_ref, gpost_ref, wg_ref, wu_ref, wd_ref, o_ref, h0, h1, acc0,
                acc1, *, n_tiles, rows_per_step):
    def matmul_step(h):
        p = _silu(_dot(h, wg_ref[...])) * _dot(h, wu_ref[...])
        return _dot(p.astype(_BF16), wd_ref[...])

    _skewed_sublayer(pl.program_id(0), pl.program_id(1), n_tiles, rows_per_step, xn_ref, xp_ref, modn_ref,
                     modp_ref, gpre_ref, gpost_ref, o_ref, (h0, h1), (acc0, acc1), 3, matmul_step)


def _ffn(x2, mod, lw, *, tm, th=512):
    m, d = x2.shape
    hidden = lw["w_ffn_gate"].shape[1]
    nj = hidden // th
    n_tiles = m // tm
    sk = _Skew(n_tiles, nj, (m // mod.shape[0]) // tm)
    return pl.pallas_call(
        functools.partial(_ffn_kernel, n_tiles=n_tiles, rows_per_step=_rows_per_step(tm, nj)),
        grid=(n_tiles + 1, nj),
        in_specs=[
            sk.tile_spec((tm, d), sk.nxt),
            sk.tile_spec((tm, d), sk.prev),
            sk.mod_spec(d, sk.nxt),
            sk.mod_spec(d, sk.prev),
            pl.BlockSpec((1, d), lambda s, j: (0, 0)),
            pl.BlockSpec((1, d), lambda s, j: (0, 0)),
            pl.BlockSpec((d, th), lambda s, j: (0, sk.col(s, j))),
            pl.BlockSpec((d, th), lambda s, j: (0, sk.col(s, j))),
            pl.BlockSpec((th, d), lambda s, j: (sk.col(s, j), 0)),
        ],
        out_specs=sk.tile_spec((tm, d), sk.prev),
        out_shape=jax.ShapeDtypeStruct((m, d), _F32),
        scratch_shapes=[pltpu.VMEM((tm, d), _BF16)] * 2 + [pltpu.VMEM((tm, d), _F32)] * 2,
        compiler_params=_params(("arbitrary", "arbitrary")),
        name="ffn",
    )(x2, x2, mod, mod, lw["g_pre_ffn"], lw["g_post_ffn"], lw["w_ffn_gate"], lw["w_ffn_up"], lw["w_ffn_down"])


def _prep_layer_weights(l, norm_pre_mix, norm_post_mix, norm_pre_ffn, norm_post_ffn, w_in, w_a2_fwd, b_a_fwd,
                        w_a2_bwd, b_a_bwd, gla_out_norm, w_br_gla, w_br_four, w_gate, b_gate, w_out,
                        w_ffn_gate, w_ffn_up, w_ffn_down):
    nh, dk, dv, rank = GLA_HEADS, GLA_DK, GLA_DV, GLA_RANK
    d = w_in.shape[1]
    kw, vw = nh * dk, nh * dv
    wi = w_in[l]
    q_k = wi[:, :2 * kw]
    v = wi[:, 2 * kw:2 * kw + vw]
    g = wi[:, 2 * kw + vw:2 * kw + 2 * vw]
    lr0 = 2 * kw + 2 * vw
    fo = wi[:, lr0 + 2 * rank:]
    w_main = jnp.concatenate([q_k, g, v, fo], axis=1).astype(_BF16)
    w_lr = jnp.pad(wi[:, lr0:lr0 + 2 * rank], ((0, 0), (0, LANES - 2 * rank))).astype(_BF16)

    def decay_map(w_a2, row0):
        return jnp.pad(w_a2, ((row0, LANES - rank - row0), (0, 0))).astype(_BF16)

    return dict(
        g_pre_mix=norm_pre_mix[l].reshape(1, d), g_post_mix=norm_post_mix[l].reshape(1, d),
        g_pre_ffn=norm_pre_ffn[l].reshape(1, d), g_post_ffn=norm_post_ffn[l].reshape(1, d),
        w_main=w_main, w_lr=w_lr, n32=2 * kw + vw, fo_blk=vw // (FOURIER_GROUPS * FOURIER_GROUP_DIM),
        wa_f=decay_map(w_a2_fwd[l], 0), wa_b=decay_map(w_a2_bwd[l], rank),
        ba_f=b_a_fwd[l].reshape(1, kw), ba_b=b_a_bwd[l].reshape(1, kw),
        onorm=gla_out_norm[l].reshape(1, vw),
        w_br_gla=w_br_gla[l].astype(_BF16), w_br_four=w_br_four[l].astype(_BF16),
        w_gate=w_gate[l].astype(_BF16), b_gate=b_gate[l].reshape(1, -1), w_out=w_out[l].astype(_BF16),
        w_ffn_gate=w_ffn_gate[l].astype(_BF16), w_ffn_up=w_ffn_up[l].astype(_BF16),
        w_ffn_down=w_ffn_down[l].astype(_BF16),
    )


def _layer(x, mod, s0f, s0b, lw, *, emit_state):
    bsz, t, d = x.shape
    m = bsz * t
    rows_per_mod = m // mod.shape[0]
    tm = min(512, rows_per_mod)
    x2 = x.reshape(m, d)
    a32, b16, lr = _in_proj(x2, mod, lw["g_pre_mix"], lw["w_main"], lw["w_lr"], tm=min(1024, rows_per_mod),
                            n32=lw["n32"])
    b16 = b16.reshape(bsz, t, -1)
    on, sf, sb = _gla(a32.reshape(bsz, t, -1), b16, lr.reshape(bsz, t, LANES), lw, s0f, s0b, emit_state=emit_state)
    f = _fnet(b16, fo_blk=lw["fo_blk"])
    x1 = _merge(x2, mod, on.reshape(m, -1), f.reshape(m, -1), lw, tm=tm)
    y = _ffn(x1, mod, lw, tm=tm)
    return y.reshape(bsz, t, d), sf, sb


def kernel(x_prompt, x_sample, state_gla_fwd, state_gla_bwd, c, c_ctx, w_ada, b_ada, norm_pre_mix, norm_post_mix,
           norm_pre_ffn, norm_post_ffn, w_in, w_a2_fwd, b_a_fwd, w_a2_bwd, b_a_bwd, gla_out_norm, w_br_gla,
           w_br_four, w_gate, b_gate, w_out, w_ffn_gate, w_ffn_up, w_ffn_down):
    depth = w_ada.shape[0]
    d = x_prompt.shape[-1]
    n_lat = c.shape[0]
    cond_rows = -(-(1 + n_lat) // 8) * 8
    cond = jnp.concatenate([c_ctx[None, :], c, jnp.zeros((cond_rows - 1 - n_lat, d), _F32)], axis=0)
    xp, xs = x_prompt, x_sample
    new_f, new_b = [], []
    for l in range(depth):
        lw = _prep_layer_weights(l, norm_pre_mix, norm_post_mix, norm_pre_ffn, norm_post_ffn, w_in, w_a2_fwd,
                                 b_a_fwd, w_a2_bwd, b_a_bwd, gla_out_norm, w_br_gla, w_br_four, w_gate, b_gate,
                                 w_out, w_ffn_gate, w_ffn_up, w_ffn_down)
        mod = _ada(cond, w_ada[l], b_ada[l])
        mod_ctx = mod[0:1].reshape(1, N_MOD, d)
        mod_lat = mod[1:1 + n_lat].reshape(n_lat, N_MOD, d)
        xp, s_f, s_b = _layer(xp, mod_ctx, None, None, lw, emit_state=True)
        new_f.append(s_f)
        new_b.append(s_b)
        xs, _, _ = _layer(xs, mod_lat, state_gla_fwd[:, l], state_gla_bwd[:, l], lw, emit_state=False)
    return (xp, xs, jnp.stack(new_f, axis=1), jnp.stack(new_b, axis=1))
```
